```python
import math
import jax, jax.numpy as jnp
from jax import lax
import numpy as np

D_MODEL = 1024
BATCH = 8
SEQ = 4096
DEPTH = 1

HEAD_DIM = 64
FOX_HEADS = 8
SWA_HEADS = 8
SWA_KV_HEADS = 2
SWA_GROUP = SWA_HEADS // SWA_KV_HEADS
FOX_W = FOX_HEADS * HEAD_DIM
SWA_QW = SWA_HEADS * HEAD_DIM
SWA_KVW = SWA_KV_HEADS * HEAD_DIM
MIX_WIDTH = FOX_W + SWA_QW
WINDOW = 128
Q_BLOCK = 128
ROPE_THETA = 10000.0

OFF_FQ = 0
OFF_FK = OFF_FQ + FOX_W
OFF_FV = OFF_FK + FOX_W
OFF_FF = OFF_FV + FOX_W
OFF_SQ = OFF_FF + FOX_HEADS
OFF_SK = OFF_SQ + SWA_QW
OFF_SV = OFF_SK + SWA_KVW
IN_COLS = OFF_SV + SWA_KVW

PEER_HEADS = 8
N_KEYS = 128
N_EXPERTS = N_KEYS * N_KEYS
PEER_DKEY = 256
PEER_HALF = PEER_DKEY // 2
PEER_TOPK = 16
PEER_TOK_BLOCK = 128

LN_EPS = 1e-5
ALPHA = (2.0 * DEPTH) ** 0.25
BETA = (8.0 * DEPTH) ** -0.25

kernel_name = "hymba_fox_swa_sink_peer_deepnorm"


def layer_norm(x, g, b):
    xf = x.astype(jnp.float32)
    mu = jnp.mean(xf, axis=-1, keepdims=True)
    var = jnp.mean(jnp.square(xf - mu), axis=-1, keepdims=True)
    return ((xf - mu) * lax.rsqrt(var + LN_EPS) * g.astype(jnp.float32) + b.astype(jnp.float32)).astype(x.dtype)


def rope(x, pos):
    half = HEAD_DIM // 2
    inv = ROPE_THETA ** (-jnp.arange(half, dtype=jnp.float32) / half)
    ang = pos.astype(jnp.float32)[:, None] * inv[None, :]
    cos = jnp.cos(ang)[None, :, None, :]
    sin = jnp.sin(ang)[None, :, None, :]
    xf = x.astype(jnp.float32)
    x1, x2 = xf[..., :half], xf[..., half:]
    return jnp.concatenate([x1 * cos - x2 * sin, x2 * cos + x1 * sin], axis=-1).astype(x.dtype)


def forgetting_attention(q, k, v, f_logit):
    B, S, H, D = q.shape
    c = jnp.cumsum(jax.nn.log_sigmoid(f_logit.astype(jnp.float32)), axis=1)
    c = jnp.transpose(c, (0, 2, 1))
    scale = D ** -0.5
    outs = []
    for i in range(S // Q_BLOCK):
        q0, q1 = i * Q_BLOCK, (i + 1) * Q_BLOCK
        s = jnp.einsum('bqhd,bkhd->bhqk', q[:, q0:q1], k[:, :q1],
                       preferred_element_type=jnp.float32) * scale
        bias = c[:, :, q0:q1, None] - c[:, :, None, :q1]
        causal = (q0 + jnp.arange(Q_BLOCK))[:, None] >= jnp.arange(q1)[None, :]
        s = jnp.where(causal[None, None], s + bias, -jnp.inf)
        p = jax.nn.softmax(s, axis=-1)
        outs.append(jnp.einsum('bhqk,bkhd->bqhd', p.astype(v.dtype), v[:, :q1]))
    return jnp.concatenate(outs, axis=1)


def sliding_window_attention(q, k, v, sinks):
    B, S, H, D = q.shape
    nb = S // WINDOW
    scale = D ** -0.5
    qb = q.reshape(B, nb, WINDOW, SWA_KV_HEADS, SWA_GROUP, D)

    def banded(t):
        tb = t.reshape(B, nb, WINDOW, SWA_KV_HEADS, D)
        prev = jnp.pad(tb, ((0, 0), (1, 0), (0, 0), (0, 0), (0, 0)))[:, :-1]
        return jnp.concatenate([prev, tb], axis=2)

    kb, vb = banded(k), banded(v)
    s = jnp.einsum('bnqkgd,bnjkd->bnkgqj', qb, kb, preferred_element_type=jnp.float32) * scale
    r = jnp.arange(WINDOW)[:, None]
    j = jnp.arange(2 * WINDOW)[None, :]
    diff = r + WINDOW - j
    in_window = (diff >= 0) & (diff < WINDOW)
    key_pos = jnp.arange(nb)[:, None] * WINDOW - WINDOW + jnp.arange(2 * WINDOW)[None, :]
    mask = in_window[None] & (key_pos >= 0)[:, None, :]
    s = jnp.where(mask[None, :, None, None], s, -jnp.inf)
    sink = sinks.astype(jnp.float32).reshape(SWA_KV_HEADS, SWA_GROUP)[None, None, :, :, None, None]
    sink = jnp.broadcast_to(sink, s.shape[:-1] + (1,))
    p = jax.nn.softmax(jnp.concatenate([s, sink], axis=-1), axis=-1)[..., :-1]
    o = jnp.einsum('bnkgqj,bnjkd->bnqkgd', p.astype(v.dtype), vb)
    return o.reshape(B, S, H, D)


def peer_ffn(x, w_query, sub_keys, expert_u, expert_v):
    B, S, Dm = x.shape
    xt = x.reshape((B * S) // PEER_TOK_BLOCK, PEER_TOK_BLOCK, Dm)

    def block(xb):
        tb = xb.shape[0]
        q = (xb @ w_query).reshape(tb, PEER_HEADS, 2, PEER_HALF)
        s = jnp.einsum('thpc,hpnc->thpn', q, sub_keys, preferred_element_type=jnp.float32)
        top_s, top_i = lax.top_k(s, PEER_TOPK)
        cand_s = (top_s[:, :, 0, :, None] + top_s[:, :, 1, None, :]).reshape(tb, PEER_HEADS, PEER_TOPK * PEER_TOPK)
        cand_i = (top_i[:, :, 0, :, None] * N_KEYS + top_i[:, :, 1, None, :]).reshape(tb, PEER_HEADS, PEER_TOPK * PEER_TOPK)
        best_s, best_pos = lax.top_k(cand_s, PEER_TOPK)
        idx = jnp.take_along_axis(cand_i, best_pos, axis=-1)
        g = jax.nn.softmax(best_s, axis=-1)
        u = expert_u[idx]
        h = jax.nn.gelu(jnp.einsum('thkd,td->thk', u, xb, preferred_element_type=jnp.float32), approximate=False)
        w = (g * h).astype(xb.dtype)
        return jnp.einsum('thk,thkd->td', w, expert_v[idx])

    return lax.map(block, xt).reshape(B, S, Dm)


def setup_inputs(seed: int = 0) -> dict:
    key = jax.random.key(seed)
    ks = jax.random.split(key, 14)
    f32 = jnp.float32
    nrm = lambda k, shape: jax.random.normal(k, shape, dtype=f32)
    x = nrm(ks[0], (BATCH, SEQ, D_MODEL))
    w_in = nrm(ks[1], (DEPTH, D_MODEL, IN_COLS)) * D_MODEL ** -0.5
    b_forget = 0.1 * nrm(ks[2], (DEPTH, FOX_HEADS))
    swa_sinks = 0.1 * nrm(ks[3], (DEPTH, SWA_HEADS))
    w_out = nrm(ks[4], (DEPTH, MIX_WIDTH, D_MODEL)) * MIX_WIDTH ** -0.5 * BETA
    ln1_g = 1.0 + 0.01 * nrm(ks[5], (DEPTH, D_MODEL))
    ln1_b = 0.01 * nrm(ks[6], (DEPTH, D_MODEL))
    w_query = nrm(ks[7], (DEPTH, D_MODEL, PEER_HEADS * PEER_DKEY)) * D_MODEL ** -0.5
    sub_keys = nrm(ks[8], (DEPTH, PEER_HEADS, 2, N_KEYS, PEER_HALF)) * PEER_HALF ** -0.5
    expert_u = nrm(ks[9], (DEPTH, N_EXPERTS, D_MODEL)) * D_MODEL ** -0.5
    expert_v = nrm(ks[10], (DEPTH, N_EXPERTS, D_MODEL)) * BETA
    ln2_g = 1.0 + 0.01 * nrm(ks[11], (DEPTH, D_MODEL))
    ln2_b = 0.01 * nrm(ks[12], (DEPTH, D_MODEL))
    return {"x": x, "w_in": w_in, "b_forget": b_forget, "swa_sinks": swa_sinks, "w_out": w_out,
            "ln1_g": ln1_g, "ln1_b": ln1_b, "w_query": w_query, "sub_keys": sub_keys,
            "expert_u": expert_u, "expert_v": expert_v, "ln2_g": ln2_g, "ln2_b": ln2_b}


def reference(x, w_in, b_forget, swa_sinks, w_out, ln1_g, ln1_b, w_query, sub_keys,
              expert_u, expert_v, ln2_g, ln2_b):
    B, S, _ = x.shape
    pos = jnp.arange(S)
    for l in range(DEPTH):
        proj = x @ w_in[l]
        fq = proj[..., OFF_FQ:OFF_FK].reshape(B, S, FOX_HEADS, HEAD_DIM)
        fk = proj[..., OFF_FK:OFF_FV].reshape(B, S, FOX_HEADS, HEAD_DIM)
        fv = proj[..., OFF_FV:OFF_FF].reshape(B, S, FOX_HEADS, HEAD_DIM)
        ff = proj[..., OFF_FF:OFF_SQ] + b_forget[l]
        sq = rope(proj[..., OFF_SQ:OFF_SK].reshape(B, S, SWA_HEADS, HEAD_DIM), pos)
        sk = rope(proj[..., OFF_SK:OFF_SV].reshape(B, S, SWA_KV_HEADS, HEAD_DIM), pos)
        sv = proj[..., OFF_SV:IN_COLS].reshape(B, S, SWA_KV_HEADS, HEAD_DIM)
        fox_out = forgetting_attention(fq, fk, fv, ff).reshape(B, S, FOX_W)
        swa_out = sliding_window_attention(sq, sk, sv, swa_sinks[l]).reshape(B, S, SWA_QW)
        mix = jnp.concatenate([fox_out, swa_out], axis=-1) @ w_out[l]
        x = layer_norm(ALPHA * x + mix, ln1_g[l], ln1_b[l])
        ffn = peer_ffn(x, w_query[l], sub_keys[l], expert_u[l], expert_v[l])
        x = layer_norm(ALPHA * x + ffn, ln2_g[l], ln2_b[l])
    return x
```

```python
import functools
import math

import jax
import jax.numpy as jnp
from jax import lax
from jax.experimental import pallas as pl
from jax.experimental.pallas import tpu as pltpu

F32 = jnp.float32
BF16 = jnp.bfloat16

D_MODEL = 1024
HEAD_DIM = 64
FOX_HEADS = 8
SWA_HEADS = 8
SWA_KV_HEADS = 2
FOX_W = FOX_HEADS * HEAD_DIM
SWA_QW = SWA_HEADS * HEAD_DIM
SWA_KVW = SWA_KV_HEADS * HEAD_DIM
WINDOW = 128
ROPE_THETA = 10000.0
OFF_FF = 3 * FOX_W
OFF_SQ = OFF_FF + FOX_HEADS
PROJ_W = 3 * FOX_W + SWA_QW + 2 * SWA_KVW
PEER_HEADS = 8
N_KEYS = 128
PEER_HALF = 128
PEER_TOPK = 16
LN_EPS = 1e-5
DEPTH = 1
ALPHA = (2.0 * DEPTH) ** 0.25
SCALE = HEAD_DIM ** -0.5
NEG_INF = float("-inf")

LANES = 128
VMEM_LIMIT = 56 * 1024 * 1024


def _params(semantics):
    return pltpu.CompilerParams(dimension_semantics=semantics, vmem_limit_bytes=VMEM_LIMIT)


def _rope_block(r, cos, sin_signed, lo_half):
    partner = jnp.where(lo_half, pltpu.roll(r, LANES - 32, axis=1), pltpu.roll(r, 32, axis=1))
    return r * cos + partner * sin_signed


def _inproj_kernel(x_ref, w_ref, cos_ref, sin_ref, o_ref):
    xb = x_ref[...].astype(BF16)
    cos = cos_ref[...]
    sin = sin_ref[...]
    lane = lax.broadcasted_iota(jnp.int32, cos.shape, 1)
    lo_half = (lane % HEAD_DIM) < (HEAD_DIM // 2)
    sq0 = 3 * FOX_W
    for c0 in range(0, PROJ_W, 512):
        c1 = min(c0 + 512, PROJ_W)
        res = jnp.dot(xb, w_ref[:, c0:c1], preferred_element_type=F32)
        for b0 in range(0, c1 - c0, LANES):
            col = c0 + b0
            r = res[:, b0:b0 + LANES]
            if col < FOX_W:
                r = r * SCALE
            elif sq0 <= col < sq0 + SWA_QW:
                r = _rope_block(r, cos, sin, lo_half) * SCALE
            elif sq0 + SWA_QW <= col < sq0 + SWA_QW + SWA_KVW:
                r = _rope_block(r, cos, sin, lo_half)
            o_ref[:, col:col + LANES] = r.astype(BF16)


def _inproj(x2, w_main, cos_t, sin_t, seq, tm):
    tokens = x2.shape[0]
    per_seq = seq // tm
    return pl.pallas_call(
        _inproj_kernel,
        grid=(tokens // tm,),
        in_specs=[
            pl.BlockSpec((tm, D_MODEL), lambda i: (i, 0)),
            pl.BlockSpec((D_MODEL, PROJ_W), lambda i: (0, 0)),
            pl.BlockSpec((tm, LANES), lambda i: (i % per_seq, 0)),
            pl.BlockSpec((tm, LANES), lambda i: (i % per_seq, 0)),
        ],
        out_specs=pl.BlockSpec((tm, PROJ_W), lambda i: (i, 0)),
        out_shape=jax.ShapeDtypeStruct((tokens, PROJ_W), BF16),
        compiler_params=_params(("parallel",)),
        name="inproj",
    )(x2, w_main, cos_t, sin_t)


def _gate_kernel(x_ref, w_ref, b_ref, tri_ref, c_ref, carry_ref):
    @pl.when(pl.program_id(1) == 0)
    def _():
        carry_ref[...] = jnp.zeros_like(carry_ref)

    z = jnp.dot(x_ref[0], w_ref[...], preferred_element_type=F32,
                precision=lax.Precision.HIGHEST) + b_ref[...]
    ls = jax.nn.log_sigmoid(z)
    ls_t = ls.T[0:FOX_HEADS, :]
    cs = jnp.dot(ls_t, tri_ref[...], preferred_element_type=F32,
                 precision=lax.Precision.HIGHEST) + carry_ref[:, 0:1]
    c_ref[0] = cs
    ts = cs.shape[1]
    carry_ref[...] = jnp.broadcast_to(cs[:, ts - 1:ts], carry_ref.shape)


def _gates(x3, w_ff, b_ff, ts):
    batch, seq, _ = x3.shape
    tri = (jnp.arange(ts)[:, None] <= jnp.arange(ts)[None, :]).astype(F32)
    return pl.pallas_call(
        _gate_kernel,
        grid=(batch, seq // ts),
        in_specs=[
            pl.BlockSpec((1, ts, D_MODEL), lambda b, j: (b, j, 0)),
            pl.BlockSpec((D_MODEL, LANES), lambda b, j: (0, 0)),
            pl.BlockSpec((1, LANES), lambda b, j: (0, 0)),
            pl.BlockSpec((ts, ts), lambda b, j: (0, 0)),
        ],
        out_specs=pl.BlockSpec((1, FOX_HEADS, ts), lambda b, j: (b, 0, j)),
        out_shape=jax.ShapeDtypeStruct((batch, FOX_HEADS, seq), F32),
        scratch_shapes=[pltpu.VMEM((FOX_HEADS, LANES), F32)],
        compiler_params=_params(("arbitrary", "arbitrary")),
        name="forget_gates",
    )(x3, w_ff, b_ff, tri)


def _fox_kernel(q_ref, k_ref, v_ref, ca_ref, cb_ref, o_ref, *, tq):
    i = pl.program_id(2)
    q = q_ref[0]
    lane = lax.broadcasted_iota(jnp.int32, q.shape, 1)
    first = lane < HEAD_DIM
    zero = jnp.zeros_like(q)
    row = lax.broadcasted_iota(jnp.int32, (tq, tq), 0)
    colv = lax.broadcasted_iota(jnp.int32, (tq, tq), 1)
    causal = row >= colv
    diag = pl.ds(pl.multiple_of(i * tq, tq), tq)

    def one_head(qm, c_ref):
        c_diag = c_ref[0, 0, :, diag]
        cref = jnp.min(c_diag, axis=1, keepdims=True)

        def step(kb, vb, bias, carry, mask):
            m, l, acc = carry
            s = lax.dot_general(qm, kb, (((1,), (1,)), ((), ())), preferred_element_type=F32)
            s = s + bias
            if mask is not None:
                s = jnp.where(mask, s, NEG_INF)
            m_new = jnp.maximum(m, jnp.max(s, axis=1, keepdims=True))
            alpha = jnp.exp(m - m_new)
            p = jnp.exp(s - m_new)
            l = alpha * l + jnp.sum(p, axis=1, keepdims=True)
            acc = alpha * acc + jnp.dot(p.astype(BF16), vb, preferred_element_type=F32)
            return m_new, l, acc

        def body(j, carry):
            ks = pl.ds(pl.multiple_of(j * tq, tq), tq)
            bias = cref - c_ref[0, 0, :, ks]
            return step(k_ref[0, ks, :], v_ref[0, ks, :], bias, carry, None)

        init = (jnp.full((tq, 1), NEG_INF, F32), jnp.zeros((tq, 1), F32),
                jnp.zeros((tq, LANES), F32))
        carry = step(k_ref[0, diag, :], v_ref[0, diag, :], cref - c_diag, init, causal)
        m, l, acc = lax.fori_loop(0, i, body, carry)
        return acc / l

    out_a = one_head(jnp.where(first, q, zero), ca_ref)
    out_b = one_head(jnp.where(first, zero, q), cb_ref)
    o_ref[0] = jnp.where(first, out_a, out_b).astype(o_ref.dtype)


def _fox(proj3, c4, tq):
    batch, seq, _ = proj3.shape
    pairs = FOX_HEADS // 2
    kv_spec = lambda off: pl.BlockSpec((1, seq, LANES), lambda b, h, i: (b, 0, off + h))
    c_spec = lambda o: pl.BlockSpec((1, 1, 1, seq), lambda b, h, i: (b, 2 * h + o, 0, 0))
    return pl.pallas_call(
        functools.partial(_fox_kernel, tq=tq),
        grid=(batch, pairs, seq // tq),
        in_specs=[
            pl.BlockSpec((1, tq, LANES), lambda b, h, i: (b, i, h)),
            kv_spec(pairs), kv_spec(2 * pairs), c_spec(0), c_spec(1),
        ],
        out_specs=pl.BlockSpec((1, tq, LANES), lambda b, h, i: (b, i, h)),
        out_shape=jax.ShapeDtypeStruct((batch, seq, FOX_W), BF16),
        compiler_params=_params(("parallel", "parallel", "arbitrary")),
        name="fox_attention",
    )(proj3, proj3, proj3, c4, c4)


def _swa_kernel(sink_ref, q_ref, kp_ref, kc_ref, vp_ref, vc_ref, o_ref):
    n = pl.program_id(1)
    kcat = jnp.concatenate([kp_ref[0], kc_ref[0]], axis=0)
    vcat = jnp.concatenate([vp_ref[0], vc_ref[0]], axis=0)
    r = lax.broadcasted_iota(jnp.int32, (WINDOW, 2 * WINDOW), 0)
    j = lax.broadcasted_iota(jnp.int32, (WINDOW, 2 * WINDOW), 1)
    diff = r + WINDOW - j
    first_key = jnp.where(n > 0, 0, WINDOW)
    mask = (diff >= 0) & (diff < WINDOW) & (j >= first_key)
    lane = lax.broadcasted_iota(jnp.int32, (WINDOW, LANES), 1)
    first = lane < HEAD_DIM
    group = SWA_HEADS // SWA_KV_HEADS
    for cb in range(SWA_QW // LANES):
        q32 = q_ref[0, :, cb * LANES:(cb + 1) * LANES].astype(F32)
        q_sw = pltpu.roll(q32, HEAD_DIM, axis=1)
        outs = []
        for half in range(2):
            h = 2 * cb + half
            g = h // group
            src = q32 if half == g else q_sw
            keep = first if g == 0 else jnp.logical_not(first)
            qm = jnp.where(keep, src, 0.0).astype(BF16)
            s = lax.dot_general(qm, kcat, (((1,), (1,)), ((), ())), preferred_element_type=F32)
            s = jnp.where(mask, s, NEG_INF)
            sink = sink_ref[h]
            m = jnp.maximum(jnp.max(s, axis=1, keepdims=True), sink)
            p = jnp.exp(s - m)
            denom = jnp.sum(p, axis=1, keepdims=True) + jnp.exp(sink - m)
            o = jnp.dot(p.astype(BF16), vcat, preferred_element_type=F32) / denom
            if half != g:
                o = pltpu.roll(o, HEAD_DIM, axis=1)
            outs.append(o)
        o_ref[0, :, cb * LANES:(cb + 1) * LANES] = jnp.where(first, outs[0], outs[1]).astype(o_ref.dtype)


def _swa(proj3, sinks):
    batch, seq, _ = proj3.shape
    q_blk = (3 * FOX_W) // SWA_QW
    k_blk = (3 * FOX_W + SWA_QW) // LANES
    v_blk = k_blk + 1
    prev = lambda blk: pl.BlockSpec((1, WINDOW, LANES), lambda b, n: (b, jnp.maximum(n - 1, 0), blk))
    cur = lambda blk: pl.BlockSpec((1, WINDOW, LANES), lambda b, n: (b, n, blk))
    return pl.pallas_call(
        _swa_kernel,
        grid=(batch, seq // WINDOW),
        in_specs=[
            pl.BlockSpec(memory_space=pltpu.SMEM),
            pl.BlockSpec((1, WINDOW, SWA_QW), lambda b, n: (b, n, q_blk)),
            prev(k_blk), cur(k_blk), prev(v_blk), cur(v_blk),
        ],
        out_specs=pl.BlockSpec((1, WINDOW, SWA_QW), lambda b, n: (b, n, 0)),
        out_shape=jax.ShapeDtypeStruct((batch, seq, SWA_QW), BF16),
        compiler_params=_params(("parallel", "arbitrary")),
        name="swa_attention",
    )(sinks, proj3, proj3, proj3, proj3, proj3)


def _outproj_kernel(fox_ref, swa_ref, x_ref, w1_ref, w2_ref, g_ref, b_ref, o_ref):
    mix = jnp.dot(fox_ref[...], w1_ref[...], preferred_element_type=F32)
    mix = mix + jnp.dot(swa_ref[...], w2_ref[...], preferred_element_type=F32)
    y = ALPHA * x_ref[...] + mix
    mu = jnp.mean(y, axis=1, keepdims=True)
    d = y - mu
    var = jnp.mean(d * d, axis=1, keepdims=True)
    x1 = d * lax.rsqrt(var + LN_EPS) * g_ref[...] + b_ref[...]
    o_ref[...] = x1.T


def _outproj(fox2, swa2, x2, w_fox, w_swa, g, b, tm):
    tokens = x2.shape[0]
    return pl.pallas_call(
        _outproj_kernel,
        grid=(tokens // tm,),
        in_specs=[
            pl.BlockSpec((tm, FOX_W), lambda i: (i, 0)),
            pl.BlockSpec((tm, SWA_QW), lambda i: (i, 0)),
            pl.BlockSpec((tm, D_MODEL), lambda i: (i, 0)),
            pl.BlockSpec((FOX_W, D_MODEL), lambda i: (0, 0)),
            pl.BlockSpec((SWA_QW, D_MODEL), lambda i: (0, 0)),
            pl.BlockSpec((1, D_MODEL), lambda i: (0, 0)),
            pl.BlockSpec((1, D_MODEL), lambda i: (0, 0)),
        ],
        out_specs=pl.BlockSpec((D_MODEL, tm), lambda i: (0, i)),
        out_shape=jax.ShapeDtypeStruct((D_MODEL, tokens), F32),
        compiler_params=_params(("parallel",)),
        name="outproj_ln1",
    )(fox2, swa2, x2, w_fox, w_swa, g, b)


def _top_values(s, out_ref):
    for r in range(PEER_TOPK):
        m = jnp.max(s, axis=0, keepdims=True)
        out_ref[r:r + 1, :] = m
        s = jnp.where(s == m, NEG_INF, s)


def _route_kernel(x_ref, wq_ref, keys_ref, thr_ref, e0_ref, s1_ref, e1_ref, v0_ref, v1_ref, c_ref):
    xb = x_ref[...].astype(BF16)
    tt = xb.shape[1]
    sub = lax.broadcasted_iota(jnp.int32, (8, tt), 0)
    for h in range(PEER_HEADS):
        sc = []
        for p in range(2):
            hp = 2 * h + p
            q = jnp.dot(wq_ref[hp * PEER_HALF:(hp + 1) * PEER_HALF, :], xb,
                        preferred_element_type=F32)
            sc.append(jnp.dot(keys_ref[hp], q.astype(BF16), preferred_element_type=F32))
        s0, s1 = sc
        _top_values(s0, v0_ref)
        _top_values(s1, v1_ref)
        n_rows = 0
        for b in range(PEER_TOPK):
            na = PEER_TOPK // (b + 1)
            vb = v1_ref[b:b + 1, :]
            if na > 8:
                c_ref[n_rows:n_rows + 16, :] = v0_ref[...] + vb
                n_rows += 16
            else:
                c_ref[n_rows:n_rows + 8, :] = jnp.where(sub < na, v0_ref[0:8, :] + vb, NEG_INF)
                n_rows += 8
        cand = c_ref[0:n_rows, :]
        work = cand
        for r in range(PEER_TOPK):
            tau = jnp.max(work, axis=0, keepdims=True)
            work = jnp.where(work == tau, NEG_INF, work)
        m0 = v0_ref[0:1, :]
        m1 = v1_ref[0:1, :]
        z = jnp.sum(jnp.where(cand >= tau, jnp.exp(cand - (m0 + m1)), 0.0), axis=0, keepdims=True)
        thr_ref[h] = tau - s0
        e0_ref[h] = jnp.exp(s0 - m0) / z
        s1_ref[h] = s1
        e1_ref[h] = jnp.exp(s1 - m1)


def _route(x1t, wq_t, keys, tt):
    tokens = x1t.shape[1]
    side = jax.ShapeDtypeStruct((PEER_HEADS, N_KEYS, tokens), F32)
    side_spec = pl.BlockSpec((PEER_HEADS, N_KEYS, tt), lambda i: (0, 0, i))
    n_cand_rows = 16 + 8 * (PEER_TOPK - 1)
    return pl.pallas_call(
        _route_kernel,
        grid=(tokens // tt,),
        in_specs=[
            pl.BlockSpec((D_MODEL, tt), lambda i: (0, i)),
            pl.BlockSpec(wq_t.shape, lambda i: (0, 0)),
            pl.BlockSpec(keys.shape, lambda i: (0, 0, 0)),
        ],
        out_specs=[side_spec] * 4,
        out_shape=[side] * 4,
        scratch_shapes=[pltpu.VMEM((PEER_TOPK, tt), F32), pltpu.VMEM((PEER_TOPK, tt), F32),
                        pltpu.VMEM((n_cand_rows, tt), F32)],
        compiler_params=_params(("parallel",)),
        name="peer_route",
    )(x1t, wq_t, keys)


def _peer_kernel(x_ref, thr_ref, e0_ref, s1_ref, e1_ref, u_ref, vt_ref, g_ref, b_ref, o_ref,
                 xb_ref, acc_ref, h_ref, w_ref, *, eb, tt):
    e = pl.program_id(1)

    @pl.when(e == 0)
    def _():
        xb_ref[...] = x_ref[...].astype(BF16)
        acc_ref[...] = jnp.zeros_like(acc_ref)

    h_ref[...] = jnp.dot(u_ref[...], xb_ref[...], preferred_element_type=F32)

    for ii in range(eb // N_KEYS):
        rows = slice(ii * N_KEYS, (ii + 1) * N_KEYS)

        def chunk(c, carry, ii=ii, rows=rows):
            lanes = pl.ds(pl.multiple_of(c * LANES, LANES), LANES)
            gate = jnp.zeros((N_KEYS, LANES), F32)
            for h in range(PEER_HEADS):
                thr = thr_ref[h, ii:ii + 1, lanes]
                e0 = e0_ref[h, ii:ii + 1, lanes]
                gate = jnp.where(s1_ref[h, :, lanes] >= thr, gate + e1_ref[h, :, lanes] * e0, gate)
            pre = h_ref[rows, lanes]
            act = 0.5 * pre * (1.0 + lax.erf(pre * math.sqrt(0.5)))
            w_ref[rows, lanes] = (gate * act).astype(BF16)
            return carry

        lax.fori_loop(0, tt // LANES, chunk, 0)

    acc_ref[...] += jnp.dot(vt_ref[...], w_ref[...], preferred_element_type=F32)

    @pl.when(e == pl.num_programs(1) - 1)
    def _():
        y = ALPHA * x_ref[...] + acc_ref[...]
        mu = jnp.mean(y, axis=0, keepdims=True)
        d = y - mu
        var = jnp.mean(d * d, axis=0, keepdims=True)
        out = d * lax.rsqrt(var + LN_EPS) * g_ref[...] + b_ref[...]
        o_ref[...] = out.T


def _peer(x1t, thr, e0, s1, e1, u_bf, vt_bf, g_col, b_col, tt, eb):
    tokens = x1t.shape[1]
    n_exp = u_bf.shape[0]
    side_spec = pl.BlockSpec((PEER_HEADS, N_KEYS, tt), lambda t, e: (0, 0, t))
    row_spec = pl.BlockSpec((PEER_HEADS, eb // N_KEYS, tt), lambda t, e: (0, e, t))
    return pl.pallas_call(
        functools.partial(_peer_kernel, eb=eb, tt=tt),
        grid=(tokens // tt, n_exp // eb),
        in_specs=[
            pl.BlockSpec((D_MODEL, tt), lambda t, e: (0, t)),
            row_spec, row_spec, side_spec, side_spec,
            pl.BlockSpec((eb, D_MODEL), lambda t, e: (e, 0)),
            pl.BlockSpec((D_MODEL, eb), lambda t, e: (0, e)),
            pl.BlockSpec((D_MODEL, 1), lambda t, e: (0, 0)),
            pl.BlockSpec((D_MODEL, 1), lambda t, e: (0, 0)),
        ],
        out_specs=pl.BlockSpec((tt, D_MODEL), lambda t, e: (t, 0)),
        out_shape=jax.ShapeDtypeStruct((tokens, D_MODEL), F32),
        scratch_shapes=[pltpu.VMEM((D_MODEL, tt), BF16), pltpu.VMEM((D_MODEL, tt), F32),
                        pltpu.VMEM((eb, tt), F32), pltpu.VMEM((eb, tt), BF16)],
        compiler_params=_params(("parallel", "arbitrary")),
        name="peer_dense",
    )(x1t, thr, e0, s1, e1, u_bf, vt_bf, g_col, b_col)


def _rope_tables(seq):
    half = HEAD_DIM // 2
    inv = ROPE_THETA ** (-jnp.arange(half, dtype=F32) / half)
    ang = jnp.arange(seq, dtype=F32)[:, None] * inv[None, :]
    cos = jnp.cos(ang)
    sin = jnp.sin(ang)
    reps = LANES // HEAD_DIM
    cos_t = jnp.tile(jnp.concatenate([cos, cos], axis=1), (1, reps))
    sin_t = jnp.tile(jnp.concatenate([-sin, sin], axis=1), (1, reps))
    return cos_t, sin_t


def _layer(x, w_in, b_forget, swa_sinks, w_out, ln1_g, ln1_b, w_query, sub_keys,
           expert_u, expert_v, ln2_g, ln2_b, *, tm, ts, tq, tt, eb):
    batch, seq, _ = x.shape
    tokens = batch * seq
    x2 = x.reshape(tokens, D_MODEL)

    w_main = jnp.concatenate([w_in[:, :OFF_FF], w_in[:, OFF_SQ:]], axis=1).astype(BF16)
    w_ff = jnp.pad(w_in[:, OFF_FF:OFF_SQ], ((0, 0), (0, LANES - FOX_HEADS)))
    b_ff = jnp.pad(b_forget, (0, LANES - FOX_HEADS)).reshape(1, LANES)
    cos_t, sin_t = _rope_tables(seq)

    proj = _inproj(x2, w_main, cos_t, sin_t, seq, tm)
    proj3 = proj.reshape(batch, seq, PROJ_W)
    c = _gates(x, w_ff, b_ff, ts)
    c4 = c.reshape(batch, FOX_HEADS, 1, seq)

    fox = _fox(proj3, c4, tq)
    swa = _swa(proj3, swa_sinks)

    w_o = w_out.astype(BF16)
    x1t = _outproj(fox.reshape(tokens, FOX_W), swa.reshape(tokens, SWA_QW), x2,
                   w_o[:FOX_W], w_o[FOX_W:], ln1_g.reshape(1, D_MODEL), ln1_b.reshape(1, D_MODEL), tm)

    wq_t = w_query.T.astype(BF16)
    keys = sub_keys.reshape(2 * PEER_HEADS, N_KEYS, PEER_HALF).astype(BF16)
    thr, e0, s1, e1 = _route(x1t, wq_t, keys, tt)

    out = _peer(x1t, thr, e0, s1, e1, expert_u.astype(BF16), expert_v.T.astype(BF16),
                ln2_g.reshape(D_MODEL, 1), ln2_b.reshape(D_MODEL, 1), tt, eb)
    return out.reshape(batch, seq, D_MODEL)


def kernel(x, w_in, b_forget, swa_sinks, w_out, ln1_g, ln1_b, w_query, sub_keys,
           expert_u, expert_v, ln2_g, ln2_b):
    assert w_in.shape[0] == DEPTH
    return _layer(x, w_in[0], b_forget[0], swa_sinks[0], w_out[0], ln1_g[0], ln1_b[0],
                  w_query[0], sub_keys[0], expert_u[0], expert_v[0], ln2_g[0], ln2_b[0],
                  tm=512, ts=512, tq=512, tt=512, eb=1024)
```

```python
import functools
import math

import jax
import jax.numpy as jnp
from jax import lax
from jax.experimental import pallas as pl
from jax.experimental.pallas import tpu as pltpu

F32 = jnp.float32
BF16 = jnp.bfloat16

D_MODEL = 1024
HEAD_DIM = 64
FOX_HEADS = 8
SWA_HEADS = 8
SWA_KV_HEADS = 2
FOX_W = FOX_HEADS * HEAD_DIM
SWA_QW = SWA_HEADS * HEAD_DIM
SWA_KVW = SWA_KV_HEADS * HEAD_DIM
WINDOW = 128
ROPE_THETA = 10000.0
OFF_FF = 3 * FOX_W
OFF_SQ = OFF_FF + FOX_HEADS
PROJ_W = 3 * FOX_W + SWA_QW + 2 * SWA_KVW
PEER_HEADS = 8
N_KEYS = 128
PEER_HALF = 128
PEER_TOPK = 16
LN_EPS = 1e-5
DEPTH = 1
ALPHA = (2.0 * DEPTH) ** 0.25
SCALE = HEAD_DIM ** -0.5
NEG_INF = float("-inf")

LANES = 128
VMEM_LIMIT = 56 * 1024 * 1024


def _params(semantics):
    return pltpu.CompilerParams(dimension_semantics=semantics, vmem_limit_bytes=VMEM_LIMIT)


def _rope_block(r, cos, sin_signed, lo_half):
    partner = jnp.where(lo_half, pltpu.roll(r, LANES - 32, axis=1), pltpu.roll(r, 32, axis=1))
    return r * cos + partner * sin_signed


def _inproj_kernel(x_ref, w_ref, cos_ref, sin_ref, o_ref):
    xb = x_ref[...].astype(BF16)
    cos = cos_ref[...]
    sin = sin_ref[...]
    lane = lax.broadcasted_iota(jnp.int32, cos.shape, 1)
    lo_half = (lane % HEAD_DIM) < (HEAD_DIM // 2)
    sq0 = 3 * FOX_W
    for c0 in range(0, PROJ_W, 512):
        c1 = min(c0 + 512, PROJ_W)
        res = jnp.dot(xb, w_ref[:, c0:c1], preferred_element_type=F32)
        for b0 in range(0, c1 - c0, LANES):
            col = c0 + b0
            r = res[:, b0:b0 + LANES]
            if col < FOX_W:
                r = r * SCALE
            elif sq0 <= col < sq0 + SWA_QW:
                r = _rope_block(r, cos, sin, lo_half) * SCALE
            elif sq0 + SWA_QW <= col < sq0 + SWA_QW + SWA_KVW:
                r = _rope_block(r, cos, sin, lo_half)
            o_ref[:, col:col + LANES] = r.astype(BF16)


def _inproj(x2, w_main, cos_t, sin_t, seq, tm):
    tokens = x2.shape[0]
    per_seq = seq // tm
    return pl.pallas_call(
        _inproj_kernel,
        grid=(tokens // tm,),
        in_specs=[
            pl.BlockSpec((tm, D_MODEL), lambda i: (i, 0)),
            pl.BlockSpec((D_MODEL, PROJ_W), lambda i: (0, 0)),
            pl.BlockSpec((tm, LANES), lambda i: (i % per_seq, 0)),
            pl.BlockSpec((tm, LANES), lambda i: (i % per_seq, 0)),
        ],
        out_specs=pl.BlockSpec((tm, PROJ_W), lambda i: (i, 0)),
        out_shape=jax.ShapeDtypeStruct((tokens, PROJ_W), BF16),
        compiler_params=_params(("parallel",)),
        name="inproj",
    )(x2, w_main, cos_t, sin_t)


def _gate_kernel(x_ref, w_ref, b_ref, tri_ref, c_ref, carry_ref):
    @pl.when(pl.program_id(1) == 0)
    def _():
        carry_ref[...] = jnp.zeros_like(carry_ref)

    z = jnp.dot(x_ref[0], w_ref[...], preferred_element_type=F32,
                precision=lax.Precision.HIGHEST) + b_ref[...]
    ls = jax.nn.log_sigmoid(z)
    ls_t = ls.T[0:FOX_HEADS, :]
    cs = jnp.dot(ls_t, tri_ref[...], preferred_element_type=F32,
                 precision=lax.Precision.HIGHEST) + carry_ref[:, 0:1]
    c_ref[0] = cs
    ts = cs.shape[1]
    carry_ref[...] = jnp.broadcast_to(cs[:, ts - 1:ts], carry_ref.shape)


def _gates(x3, w_ff, b_ff, ts):
    batch, seq, _ = x3.shape
    tri = (jnp.arange(ts)[:, None] <= jnp.arange(ts)[None, :]).astype(F32)
    return pl.pallas_call(
        _gate_kernel,
        grid=(batch, seq // ts),
        in_specs=[
            pl.BlockSpec((1, ts, D_MODEL), lambda b, j: (b, j, 0)),
            pl.BlockSpec((D_MODEL, LANES), lambda b, j: (0, 0)),
            pl.BlockSpec((1, LANES), lambda b, j: (0, 0)),
            pl.BlockSpec((ts, ts), lambda b, j: (0, 0)),
        ],
        out_specs=pl.BlockSpec((1, FOX_HEADS, ts), lambda b, j: (b, 0, j)),
        out_shape=jax.ShapeDtypeStruct((batch, FOX_HEADS, seq), F32),
        scratch_shapes=[pltpu.VMEM((FOX_HEADS, LANES), F32)],
        compiler_params=_params(("arbitrary", "arbitrary")),
        name="forget_gates",
    )(x3, w_ff, b_ff, tri)


def _fox_kernel(q_ref, k_ref, v_ref, ca_ref, cb_ref, o_ref, *, tq):
    i = pl.program_id(2)
    q = q_ref[0]
    lane = lax.broadcasted_iota(jnp.int32, q.shape, 1)
    first = lane < HEAD_DIM
    zero = jnp.zeros_like(q)
    row = lax.broadcasted_iota(jnp.int32, (tq, tq), 0)
    colv = lax.broadcasted_iota(jnp.int32, (tq, tq), 1)
    causal = row >= colv
    diag = pl.ds(pl.multiple_of(i * tq, tq), tq)

    qms = (jnp.where(first, q, zero), jnp.where(first, zero, q))
    c_refs = (ca_ref, cb_ref)
    c_diags = tuple(c_ref[0, 0, :, diag] for c_ref in c_refs)
    crefs = tuple(jnp.min(cd, axis=1, keepdims=True) for cd in c_diags)

    def step(kb, vb, biases, carries, mask):
        new = []
        for qm, bias, (m, l, acc) in zip(qms, biases, carries):
            s = lax.dot_general(qm, kb, (((1,), (1,)), ((), ())), preferred_element_type=F32)
            s = s + bias
            if mask is not None:
                s = jnp.where(mask, s, NEG_INF)
            m_new = jnp.maximum(m, jnp.max(s, axis=1, keepdims=True))
            alpha = jnp.exp(m - m_new)
            p = jnp.exp(s - m_new)
            l = alpha * l + jnp.sum(p, axis=1, keepdims=True)
            acc = alpha * acc + jnp.dot(p.astype(BF16), vb, preferred_element_type=F32)
            new.append((m_new, l, acc))
        return tuple(new)

    def body(j, carries):
        ks = pl.ds(pl.multiple_of(j * tq, tq), tq)
        biases = tuple(cref - c_ref[0, 0, :, ks] for cref, c_ref in zip(crefs, c_refs))
        return step(k_ref[0, ks, :], v_ref[0, ks, :], biases, carries, None)

    init = (jnp.full((tq, 1), NEG_INF, F32), jnp.zeros((tq, 1), F32), jnp.zeros((tq, LANES), F32))
    carries = step(k_ref[0, diag, :], v_ref[0, diag, :],
                   tuple(cref - cd for cref, cd in zip(crefs, c_diags)), (init, init), causal)
    (_, l_a, acc_a), (_, l_b, acc_b) = lax.fori_loop(0, i, body, carries)
    o_ref[0] = jnp.where(first, acc_a / l_a, acc_b / l_b).astype(o_ref.dtype)


def _fox(proj3, c4, tq):
    batch, seq, _ = proj3.shape
    pairs = FOX_HEADS // 2
    kv_spec = lambda off: pl.BlockSpec((1, seq, LANES), lambda b, h, i: (b, 0, off + h))
    c_spec = lambda o: pl.BlockSpec((1, 1, 1, seq), lambda b, h, i: (b, 2 * h + o, 0, 0))
    return pl.pallas_call(
        functools.partial(_fox_kernel, tq=tq),
        grid=(batch, pairs, seq // tq),
        in_specs=[
            pl.BlockSpec((1, tq, LANES), lambda b, h, i: (b, i, h)),
            kv_spec(pairs), kv_spec(2 * pairs), c_spec(0), c_spec(1),
        ],
        out_specs=pl.BlockSpec((1, tq, LANES), lambda b, h, i: (b, i, h)),
        out_shape=jax.ShapeDtypeStruct((batch, seq, FOX_W), BF16),
        compiler_params=_params(("parallel", "parallel", "arbitrary")),
        name="fox_attention",
    )(proj3, proj3, proj3, c4, c4)


def _swa_kernel(sink_ref, q_ref, kp_ref, kc_ref, vp_ref, vc_ref, o_ref):
    n = pl.program_id(1)
    kcat = jnp.concatenate([kp_ref[0], kc_ref[0]], axis=0)
    vcat = jnp.concatenate([vp_ref[0], vc_ref[0]], axis=0)
    r = lax.broadcasted_iota(jnp.int32, (WINDOW, 2 * WINDOW), 0)
    j = lax.broadcasted_iota(jnp.int32, (WINDOW, 2 * WINDOW), 1)
    diff = r + WINDOW - j
    first_key = jnp.where(n > 0, 0, WINDOW)
    mask = (diff >= 0) & (diff < WINDOW) & (j >= first_key)
    lane = lax.broadcasted_iota(jnp.int32, (WINDOW, LANES), 1)
    first = lane < HEAD_DIM
    group = SWA_HEADS // SWA_KV_HEADS
    for cb in range(SWA_QW // LANES):
        q32 = q_ref[0, :, cb * LANES:(cb + 1) * LANES].astype(F32)
        q_sw = pltpu.roll(q32, HEAD_DIM, axis=1)
        outs = []
        for half in range(2):
            h = 2 * cb + half
            g = h // group
            src = q32 if half == g else q_sw
            keep = first if g == 0 else jnp.logical_not(first)
            qm = jnp.where(keep, src, 0.0).astype(BF16)
            s = lax.dot_general(qm, kcat, (((1,), (1,)), ((), ())), preferred_element_type=F32)
            s = jnp.where(mask, s, NEG_INF)
            sink = sink_ref[h]
            m = jnp.maximum(jnp.max(s, axis=1, keepdims=True), sink)
            p = jnp.exp(s - m)
            denom = jnp.sum(p, axis=1, keepdims=True) + jnp.exp(sink - m)
            o = jnp.dot(p.astype(BF16), vcat, preferred_element_type=F32) / denom
            if half != g:
                o = pltpu.roll(o, HEAD_DIM, axis=1)
            outs.append(o)
        o_ref[0, :, cb * LANES:(cb + 1) * LANES] = jnp.where(first, outs[0], outs[1]).astype(o_ref.dtype)


def _swa(proj3, sinks):
    batch, seq, _ = proj3.shape
    q_blk = (3 * FOX_W) // SWA_QW
    k_blk = (3 * FOX_W + SWA_QW) // LANES
    v_blk = k_blk + 1
    prev = lambda blk: pl.BlockSpec((1, WINDOW, LANES), lambda b, n: (b, jnp.maximum(n - 1, 0), blk))
    cur = lambda blk: pl.BlockSpec((1, WINDOW, LANES), lambda b, n: (b, n, blk))
    return pl.pallas_call(
        _swa_kernel,
        grid=(batch, seq // WINDOW),
        in_specs=[
            pl.BlockSpec(memory_space=pltpu.SMEM),
            pl.BlockSpec((1, WINDOW, SWA_QW), lambda b, n: (b, n, q_blk)),
            prev(k_blk), cur(k_blk), prev(v_blk), cur(v_blk),
        ],
        out_specs=pl.BlockSpec((1, WINDOW, SWA_QW), lambda b, n: (b, n, 0)),
        out_shape=jax.ShapeDtypeStruct((batch, seq, SWA_QW), BF16),
        compiler_params=_params(("parallel", "arbitrary")),
        name="swa_attention",
    )(sinks, proj3, proj3, proj3, proj3, proj3)


def _outproj_kernel(fox_ref, swa_ref, x_ref, w1_ref, w2_ref, g_ref, b_ref, o_ref):
    mix = jnp.dot(fox_ref[...], w1_ref[...], preferred_element_type=F32)
    mix = mix + jnp.dot(swa_ref[...], w2_ref[...], preferred_element_type=F32)
    y = ALPHA * x_ref[...] + mix
    mu = jnp.mean(y, axis=1, keepdims=True)
    d = y - mu
    var = jnp.mean(d * d, axis=1, keepdims=True)
    x1 = d * lax.rsqrt(var + LN_EPS) * g_ref[...] + b_ref[...]
    o_ref[...] = x1.T


def _outproj(fox2, swa2, x2, w_fox, w_swa, g, b, tm):
    tokens = x2.shape[0]
    return pl.pallas_call(
        _outproj_kernel,
        grid=(tokens // tm,),
        in_specs=[
            pl.BlockSpec((tm, FOX_W), lambda i: (i, 0)),
            pl.BlockSpec((tm, SWA_QW), lambda i: (i, 0)),
            pl.BlockSpec((tm, D_MODEL), lambda i: (i, 0)),
            pl.BlockSpec((FOX_W, D_MODEL), lambda i: (0, 0)),
            pl.BlockSpec((SWA_QW, D_MODEL), lambda i: (0, 0)),
            pl.BlockSpec((1, D_MODEL), lambda i: (0, 0)),
            pl.BlockSpec((1, D_MODEL), lambda i: (0, 0)),
        ],
        out_specs=pl.BlockSpec((D_MODEL, tm), lambda i: (0, i)),
        out_shape=jax.ShapeDtypeStruct((D_MODEL, tokens), F32),
        compiler_params=_params(("parallel",)),
        name="outproj_ln1",
    )(fox2, swa2, x2, w_fox, w_swa, g, b)


def _top_values(s, out_ref, want_rank=False):
    rank = jnp.full(s.shape, float(PEER_TOPK), F32) if want_rank else None
    for r in range(PEER_TOPK):
        m = jnp.max(s, axis=0, keepdims=True)
        out_ref[r:r + 1, :] = m
        hit = s == m
        if want_rank:
            rank = jnp.where(hit, float(r), rank)
        s = jnp.where(hit, NEG_INF, s)
    return rank


def _route_kernel(x_ref, wq_ref, keys_ref, n0_ref, e0_ref, r1_ref, e1_ref, v0_ref, v1_ref, c_ref):
    xb = x_ref[...].astype(BF16)
    tt = xb.shape[1]
    sub = lax.broadcasted_iota(jnp.int32, (8, tt), 0)
    for h in range(PEER_HEADS):
        sc = []
        for p in range(2):
            hp = 2 * h + p
            q = jnp.dot(wq_ref[hp * PEER_HALF:(hp + 1) * PEER_HALF, :], xb,
                        preferred_element_type=F32)
            sc.append(jnp.dot(keys_ref[hp], q.astype(BF16), preferred_element_type=F32))
        s0, s1 = sc
        _top_values(s0, v0_ref)
        rank1 = _top_values(s1, v1_ref, want_rank=True)
        n_rows = 0
        for b in range(PEER_TOPK):
            na = PEER_TOPK // (b + 1)
            vb = v1_ref[b:b + 1, :]
            if na > 8:
                c_ref[n_rows:n_rows + 16, :] = v0_ref[...] + vb
                n_rows += 16
            else:
                c_ref[n_rows:n_rows + 8, :] = jnp.where(sub < na, v0_ref[0:8, :] + vb, NEG_INF)
                n_rows += 8
        cand = c_ref[0:n_rows, :]
        work = cand
        for r in range(PEER_TOPK):
            tau = jnp.max(work, axis=0, keepdims=True)
            work = jnp.where(work == tau, NEG_INF, work)
        m0 = v0_ref[0:1, :]
        m1 = v1_ref[0:1, :]
        z = jnp.sum(jnp.where(cand >= tau, jnp.exp(cand - (m0 + m1)), 0.0), axis=0, keepdims=True)
        n0 = jnp.zeros_like(s0)
        for b in range(PEER_TOPK):
            n0 = n0 + jnp.where(s0 + v1_ref[b:b + 1, :] >= tau, 1.0, 0.0)
        n0_ref[h] = n0
        e0_ref[h] = jnp.exp(s0 - m0) / z
        r1_ref[h] = rank1.astype(BF16)
        e1_ref[h] = jnp.exp(s1 - m1).astype(BF16)


def _route(x1t, wq_t, keys, tt):
    tokens = x1t.shape[1]
    side_spec = pl.BlockSpec((PEER_HEADS, N_KEYS, tt), lambda i: (0, 0, i))
    side = lambda dt: jax.ShapeDtypeStruct((PEER_HEADS, N_KEYS, tokens), dt)
    n_cand_rows = 16 + 8 * (PEER_TOPK - 1)
    return pl.pallas_call(
        _route_kernel,
        grid=(tokens // tt,),
        in_specs=[
            pl.BlockSpec((D_MODEL, tt), lambda i: (0, i)),
            pl.BlockSpec(wq_t.shape, lambda i: (0, 0)),
            pl.BlockSpec(keys.shape, lambda i: (0, 0, 0)),
        ],
        out_specs=[side_spec] * 4,
        out_shape=[side(F32), side(F32), side(BF16), side(BF16)],
        scratch_shapes=[pltpu.VMEM((PEER_TOPK, tt), F32), pltpu.VMEM((PEER_TOPK, tt), F32),
                        pltpu.VMEM((n_cand_rows, tt), F32)],
        compiler_params=_params(("parallel",)),
        name="peer_route",
    )(x1t, wq_t, keys)


EW_ROWS = 32


def _peer_kernel(x_ref, n0_ref, e0_ref, r1_ref, e1_ref, u_ref, vt_ref, g_ref, b_ref, o_ref,
                 xb_ref, acc_ref, h_ref, w_ref, *, eb):
    e = pl.program_id(1)
    tt = x_ref.shape[1]

    @pl.when(e == 0)
    def _():
        xb_ref[...] = x_ref[...].astype(BF16)
        acc_ref[...] = jnp.zeros_like(acc_ref)

    h_ref[...] = jnp.dot(u_ref[...], xb_ref[...], preferred_element_type=F32)

    n_tiles = N_KEYS // EW_ROWS
    for ii in range(eb // N_KEYS):
        gates = [None] * n_tiles
        for h in range(PEER_HEADS):
            n0 = jnp.broadcast_to(n0_ref[h, ii:ii + 1, :].astype(BF16), (EW_ROWS, tt))
            e0 = jnp.broadcast_to(e0_ref[h, ii:ii + 1, :].astype(BF16), (EW_ROWS, tt))
            for jb in range(n_tiles):
                jr = slice(jb * EW_ROWS, (jb + 1) * EW_ROWS)
                hit = r1_ref[h, jr, :] < n0
                val = e1_ref[h, jr, :] * e0
                if gates[jb] is None:
                    gates[jb] = jnp.where(hit, val, jnp.zeros_like(val))
                else:
                    gates[jb] = jnp.where(hit, gates[jb] + val, gates[jb])
        for jb in range(n_tiles):
            rows = slice(ii * N_KEYS + jb * EW_ROWS, ii * N_KEYS + (jb + 1) * EW_ROWS)
            pre = h_ref[rows, :]
            act = 0.5 * pre * (1.0 + lax.erf(pre * math.sqrt(0.5)))
            w_ref[rows, :] = gates[jb] * act.astype(BF16)

    acc_ref[...] += jnp.dot(vt_ref[...], w_ref[...], preferred_element_type=F32)

    @pl.when(e == pl.num_programs(1) - 1)
    def _():
        y = ALPHA * x_ref[...] + acc_ref[...]
        mu = jnp.mean(y, axis=0, keepdims=True)
        d = y - mu
        var = jnp.mean(d * d, axis=0, keepdims=True)
        out = d * lax.rsqrt(var + LN_EPS) * g_ref[...] + b_ref[...]
        o_ref[...] = out.T


def _peer(x1t, n0, e0, r1, e1, u_bf, vt_bf, g_col, b_col, tt, eb):
    tokens = x1t.shape[1]
    n_exp = u_bf.shape[0]
    side_spec = pl.BlockSpec((PEER_HEADS, N_KEYS, tt), lambda t, e: (0, 0, t))
    row_spec = pl.BlockSpec((PEER_HEADS, eb // N_KEYS, tt), lambda t, e: (0, e, t))
    return pl.pallas_call(
        functools.partial(_peer_kernel, eb=eb),
        grid=(tokens // tt, n_exp // eb),
        in_specs=[
            pl.BlockSpec((D_MODEL, tt), lambda t, e: (0, t)),
            row_spec, row_spec, side_spec, side_spec,
            pl.BlockSpec((eb, D_MODEL), lambda t, e: (e, 0)),
            pl.BlockSpec((D_MODEL, eb), lambda t, e: (0, e)),
            pl.BlockSpec((D_MODEL, 1), lambda t, e: (0, 0)),
            pl.BlockSpec((D_MODEL, 1), lambda t, e: (0, 0)),
        ],
        out_specs=pl.BlockSpec((tt, D_MODEL), lambda t, e: (t, 0)),
        out_shape=jax.ShapeDtypeStruct((tokens, D_MODEL), F32),
        scratch_shapes=[pltpu.VMEM((D_MODEL, tt), BF16), pltpu.VMEM((D_MODEL, tt), F32),
                        pltpu.VMEM((eb, tt), F32), pltpu.VMEM((eb, tt), BF16)],
        compiler_params=_params(("parallel", "arbitrary")),
        name="peer_dense",
    )(x1t, n0, e0, r1, e1, u_bf, vt_bf, g_col, b_col)


def _rope_tables(seq):
    half = HEAD_DIM // 2
    inv = ROPE_THETA ** (-jnp.arange(half, dtype=F32) / half)
    ang = jnp.arange(seq, dtype=F32)[:, None] * inv[None, :]
    cos = jnp.cos(ang)
    sin = jnp.sin(ang)
    reps = LANES // HEAD_DIM
    cos_t = jnp.tile(jnp.concatenate([cos, cos], axis=1), (1, reps))
    sin_t = jnp.tile(jnp.concatenate([-sin, sin], axis=1), (1, reps))
    return cos_t, sin_t


def _layer(x, w_in, b_forget, swa_sinks, w_out, ln1_g, ln1_b, w_query, sub_keys,
           expert_u, expert_v, ln2_g, ln2_b, *, tm, ts, tq, tr, tt, eb):
    batch, seq, _ = x.shape
    tokens = batch * seq
    x2 = x.reshape(tokens, D_MODEL)

    w_main = jnp.concatenate([w_in[:, :OFF_FF], w_in[:, OFF_SQ:]], axis=1).astype(BF16)
    w_ff = jnp.pad(w_in[:, OFF_FF:OFF_SQ], ((0, 0), (0, LANES - FOX_HEADS)))
    b_ff = jnp.pad(b_forget, (0, LANES - FOX_HEADS)).reshape(1, LANES)
    cos_t, sin_t = _rope_tables(seq)

    proj = _inproj(x2, w_main, cos_t, sin_t, seq, tm)
    proj3 = proj.reshape(batch, seq, PROJ_W)
    c = _gates(x, w_ff, b_ff, ts)
    c4 = c.reshape(batch, FOX_HEADS, 1, seq)

    fox = _fox(proj3, c4, tq)
    swa = _swa(proj3, swa_sinks)

    w_o = w_out.astype(BF16)
    x1t = _outproj(fox.reshape(tokens, FOX_W), swa.reshape(tokens, SWA_QW), x2,
                   w_o[:FOX_W], w_o[FOX_W:], ln1_g.reshape(1, D_MODEL), ln1_b.reshape(1, D_MODEL), tm)

    wq_t = w_query.T.astype(BF16)
    keys = sub_keys.reshape(2 * PEER_HEADS, N_KEYS, PEER_HALF).astype(BF16)
    n0, e0, r1, e1 = _route(x1t, wq_t, keys, tr)

    out = _peer(x1t, n0, e0, r1, e1, expert_u.astype(BF16), expert_v.T.astype(BF16),
                ln2_g.reshape(D_MODEL, 1), ln2_b.reshape(D_MODEL, 1), tt, eb)
    return out.reshape(batch, seq, D_MODEL)


def kernel(x, w_in, b_forget, swa_sinks, w_out, ln1_g, ln1_b, w_query, sub_keys,
           expert_u, expert_v, ln2_g, ln2_b):
    assert w_in.shape[0] == DEPTH
    return _layer(x, w_in[0], b_forget[0], swa_sinks[0], w_out[0], ln1_g[0], ln1_b[0],
                  w_query[0], sub_keys[0], expert_u[0], expert_v[0], ln2_g[0], ln2_b[0],
                  tm=512, ts=512, tq=512, tr=256, tt=512, eb=1024)
```
